```python
import math, functools
import jax, jax.numpy as jnp
from jax import lax
import numpy as np

D_MODEL = 2048
BATCH = 2
SEQ = 8192
DEPTH = 1
DEC_BATCH = 32
DEC_SEQ = 8
PAST_LEN = 16384
PAGE_SIZE = 128

H_ATT = 8
DK = 64
DV = 2 * DK
ATT_W = H_ATT * DV
QK_W = H_ATT * 2 * DK
SSM_W = D_MODEL - ATT_W
SSM_CG = 16
SSM_G = SSM_W // SSM_CG
SSM_P = 64
D_FF = 4 * D_MODEL
ALPHA = (2.0 * DEPTH) ** 0.25
BETA = (8.0 * DEPTH) ** -0.25
Q_BLOCK = 128
LN_EPS = 1e-5

kernel_name = "hybrid_s5_diffattn_decode_step"


def layer_norm(x, g, b):
    xf = x.astype(jnp.float32)
    mu = jnp.mean(xf, axis=-1, keepdims=True)
    var = jnp.mean(jnp.square(xf - mu), axis=-1, keepdims=True)
    return (xf - mu) * lax.rsqrt(var + LN_EPS) * g + b


def alibi_slopes(n_heads):
    return jnp.exp2(-8.0 * jnp.arange(1, n_heads + 1, dtype=jnp.float32) / n_heads)


def diff_attend(q, k, v, q_pos, k_pos, lam, slopes):
    s = jnp.einsum('nqhmd,nkhmd->nhmqk', q.astype(jnp.float32), k.astype(jnp.float32)) * (DK ** -0.5)
    dist = (q_pos[:, None] - k_pos[None, :]).astype(jnp.float32)
    s = s - slopes[None, :, None, None, None] * dist
    s = jnp.where(k_pos[None, :] <= q_pos[:, None], s, -jnp.inf)
    p = jax.nn.softmax(s, axis=-1)
    w = p[:, :, 0] - lam * p[:, :, 1]
    return jnp.einsum('nhqk,nkhd->nqhd', w, v.astype(jnp.float32))


def prompt_attend(q, k, v, lam, slopes):
    n, t = q.shape[0], q.shape[1]
    k_pos = jnp.arange(t, dtype=jnp.int32)

    def block(i):
        qb = lax.dynamic_slice_in_dim(q, i * Q_BLOCK, Q_BLOCK, axis=1)
        q_pos = i * Q_BLOCK + jnp.arange(Q_BLOCK, dtype=jnp.int32)
        return diff_attend(qb, k, v, q_pos, k_pos, lam, slopes)

    out = lax.map(block, jnp.arange(t // Q_BLOCK, dtype=jnp.int32))
    return jnp.moveaxis(out, 0, 1).reshape(n, t, H_ATT, DV)


def sample_attend(q, k, v, lam, slopes, cache_k_l, cache_v_l, page_table):
    n_pages = page_table.shape[1]
    page = cache_k_l.shape[1]
    past = n_pages * page
    tq = q.shape[1]
    q_pos = past + jnp.arange(tq, dtype=jnp.int32)
    k_pos = jnp.arange(past + tq, dtype=jnp.int32)

    def one(args):
        qb, kb, vb, pt = args
        kp = cache_k_l[pt].reshape(past, H_ATT, 2, DK)
        vp = cache_v_l[pt].reshape(past, H_ATT, DV)
        k_all = jnp.concatenate([kp.astype(jnp.float32), kb.astype(jnp.float32)], axis=0)
        v_all = jnp.concatenate([vp.astype(jnp.float32), vb.astype(jnp.float32)], axis=0)
        return diff_attend(qb[None], k_all[None], v_all[None], q_pos, k_pos, lam, slopes)[0]

    return lax.map(one, (q, k, v, page_table))


def s5_scan(u, h0_re, h0_im, a_re, a_im, log_dt, b_re, b_im, c_re, c_im, d_skip):
    n, t, _ = u.shape
    uf = u.astype(jnp.float32).reshape(n, t, SSM_G, SSM_CG)
    lam_re = jnp.minimum(a_re.astype(jnp.float32), -1e-4)
    lam_im = a_im.astype(jnp.float32)
    dt = jnp.exp(log_dt.astype(jnp.float32))[:, None]
    z_re, z_im = lam_re * dt, lam_im * dt
    ez = jnp.exp(z_re)
    ab_re, ab_im = ez * jnp.cos(z_im), ez * jnp.sin(z_im)
    den = lam_re * lam_re + lam_im * lam_im
    nr, ni = ab_re - 1.0, ab_im
    f_re = (nr * lam_re + ni * lam_im) / den
    f_im = (ni * lam_re - nr * lam_im) / den
    bb_re = f_re[..., None] * b_re - f_im[..., None] * b_im
    bb_im = f_re[..., None] * b_im + f_im[..., None] * b_re
    bu_re = jnp.einsum('ntgc,gpc->ntgp', uf, bb_re)
    bu_im = jnp.einsum('ntgc,gpc->ntgp', uf, bb_im)
    a_seq_re = jnp.broadcast_to(ab_re[None, None], (1, t, SSM_G, SSM_P))
    a_seq_im = jnp.broadcast_to(ab_im[None, None], (1, t, SSM_G, SSM_P))

    def combine(e1, e2):
        a1r, a1i, b1r, b1i = e1
        a2r, a2i, b2r, b2i = e2
        return (a1r * a2r - a1i * a2i,
                a1r * a2i + a1i * a2r,
                a2r * b1r - a2i * b1i + b2r,
                a2r * b1i + a2i * b1r + b2i)

    _, _, s_re, s_im = lax.associative_scan(combine, (a_seq_re, a_seq_im, bu_re, bu_im), axis=1)
    steps = jnp.arange(1, t + 1, dtype=jnp.float32)[:, None, None]
    ek = jnp.exp(z_re[None] * steps)
    p_re, p_im = ek * jnp.cos(z_im[None] * steps), ek * jnp.sin(z_im[None] * steps)
    h_re = h0_re.astype(jnp.float32)[:, None]
    h_im = h0_im.astype(jnp.float32)[:, None]
    x_re = s_re + p_re[None] * h_re - p_im[None] * h_im
    x_im = s_im + p_re[None] * h_im + p_im[None] * h_re
    y = (jnp.einsum('gcp,ntgp->ntgc', c_re, x_re) - jnp.einsum('gcp,ntgp->ntgc', c_im, x_im)
         + d_skip.reshape(SSM_G, SSM_CG) * uf)
    return y.reshape(n, t, SSM_W), x_re[:, -1], x_im[:, -1]


def hybrid_layer(x, attend, h0_re, h0_im, lam_init, w_in, lambda_q1, lambda_k1, lambda_q2, lambda_k2,
                 subln_g, ssm_a_re, ssm_a_im, ssm_log_dt, ssm_b_re, ssm_b_im, ssm_c_re, ssm_c_im,
                 ssm_d, w_glu, b_glu, w_out, ln1_g, ln1_b, w_up, w_down, ln2_g, ln2_b):
    n, t, _ = x.shape
    h = jnp.einsum('ntd,dc->ntc', x, w_in)
    q = h[..., :QK_W].reshape(n, t, H_ATT, 2, DK)
    k = h[..., QK_W:2 * QK_W].reshape(n, t, H_ATT, 2, DK)
    v = h[..., 2 * QK_W:2 * QK_W + ATT_W].reshape(n, t, H_ATT, DV)
    u = h[..., 2 * QK_W + ATT_W:]
    lam = (jnp.exp(jnp.sum(lambda_q1.astype(jnp.float32) * lambda_k1.astype(jnp.float32)))
           - jnp.exp(jnp.sum(lambda_q2.astype(jnp.float32) * lambda_k2.astype(jnp.float32))) + lam_init)
    o = attend(q, k, v, lam)
    o = o * lax.rsqrt(jnp.mean(o * o, axis=-1, keepdims=True) + LN_EPS) * subln_g * (1.0 - lam_init)
    o = o.reshape(n, t, ATT_W)
    y, hr, hi = s5_scan(u, h0_re, h0_im, ssm_a_re, ssm_a_im, ssm_log_dt, ssm_b_re, ssm_b_im,
                        ssm_c_re, ssm_c_im, ssm_d)
    g = jax.nn.gelu(y)
    s = g * jax.nn.sigmoid(jnp.einsum('ntc,cd->ntd', g, w_glu) + b_glu)
    mix = jnp.einsum('ntc,cd->ntd', jnp.concatenate([o, s], axis=-1).astype(x.dtype), w_out)
    x1 = layer_norm(ALPHA * x + mix, ln1_g, ln1_b).astype(x.dtype)
    f = jnp.einsum('ntf,fd->ntd', jnp.square(jax.nn.relu(jnp.einsum('ntd,df->ntf', x1, w_up))), w_down)
    x2 = layer_norm(ALPHA * x1 + f, ln2_g, ln2_b).astype(x.dtype)
    return x2, k.reshape(n, t, H_ATT, 2 * DK), v, hr, hi


def setup_inputs(seed: int = 0) -> dict:
    key = jax.random.key(seed)
    ks = jax.random.split(key, 40)
    f32 = jnp.float32
    n_pages = PAST_LEN // PAGE_SIZE
    n_pool = (DEC_BATCH * n_pages * 5) // 4
    nrm = lambda i, shape, scale: jax.random.normal(ks[i], shape, f32) * scale
    page_table = jax.random.permutation(ks[7], n_pool)[:DEC_BATCH * n_pages].reshape(DEC_BATCH, n_pages).astype(jnp.int32)
    a_im = jnp.pi * jnp.arange(SSM_P, dtype=f32)[None, None, :] + nrm(15, (DEPTH, SSM_G, SSM_P), 0.01)
    log_dt = jax.random.uniform(ks[16], (DEPTH, SSM_G), f32, math.log(1e-3), math.log(1e-1))
    return {
        "x_prompt": nrm(0, (BATCH, SEQ, D_MODEL), 1.0),
        "x_sample": nrm(1, (DEC_BATCH, DEC_SEQ, D_MODEL), 1.0),
        "cache_k": nrm(2, (DEPTH, n_pool, PAGE_SIZE, H_ATT, 2 * DK), 1.0),
        "cache_v": nrm(3, (DEPTH, n_pool, PAGE_SIZE, H_ATT, DV), 1.0),
        "state_ssm_re": nrm(4, (DEPTH, DEC_BATCH, SSM_G, SSM_P), 0.1),
        "state_ssm_im": nrm(5, (DEPTH, DEC_BATCH, SSM_G, SSM_P), 0.1),
        "page_table": page_table,
        "w_in": nrm(8, (DEPTH, D_MODEL, 2 * QK_W + ATT_W + SSM_W), D_MODEL ** -0.5),
        "lambda_q1": nrm(9, (DEPTH, DK), 0.1),
        "lambda_k1": nrm(10, (DEPTH, DK), 0.1),
        "lambda_q2": nrm(11, (DEPTH, DK), 0.1),
        "lambda_k2": nrm(12, (DEPTH, DK), 0.1),
        "subln_g": 1.0 + nrm(13, (DEPTH, DV), 0.01),
        "ssm_a_re": -0.5 + nrm(14, (DEPTH, SSM_G, SSM_P), 0.01),
        "ssm_a_im": a_im,
        "ssm_log_dt": log_dt,
        "ssm_b_re": nrm(17, (DEPTH, SSM_G, SSM_P, SSM_CG), (2.0 * SSM_CG) ** -0.5),
        "ssm_b_im": nrm(18, (DEPTH, SSM_G, SSM_P, SSM_CG), (2.0 * SSM_CG) ** -0.5),
        "ssm_c_re": nrm(19, (DEPTH, SSM_G, SSM_CG, SSM_P), SSM_P ** -0.5),
        "ssm_c_im": nrm(20, (DEPTH, SSM_G, SSM_CG, SSM_P), SSM_P ** -0.5),
        "ssm_d": nrm(21, (DEPTH, SSM_W), 1.0),
        "w_glu": nrm(22, (DEPTH, SSM_W, SSM_W), SSM_W ** -0.5),
        "b_glu": nrm(23, (DEPTH, SSM_W), 0.01),
        "w_out": nrm(24, (DEPTH, D_MODEL, D_MODEL), BETA * D_MODEL ** -0.5),
        "ln1_g": 1.0 + nrm(25, (DEPTH, D_MODEL), 0.01),
        "ln1_b": nrm(26, (DEPTH, D_MODEL), 0.01),
        "w_up": nrm(27, (DEPTH, D_MODEL, D_FF), D_MODEL ** -0.5),
        "w_down": nrm(28, (DEPTH, D_FF, D_MODEL), BETA * D_FF ** -0.5),
        "ln2_g": 1.0 + nrm(29, (DEPTH, D_MODEL), 0.01),
        "ln2_b": nrm(30, (DEPTH, D_MODEL), 0.01),
    }


def reference(x_prompt, x_sample, cache_k, cache_v, state_ssm_re, state_ssm_im, page_table,
              w_in, lambda_q1, lambda_k1, lambda_q2, lambda_k2, subln_g, ssm_a_re, ssm_a_im,
              ssm_log_dt, ssm_b_re, ssm_b_im, ssm_c_re, ssm_c_im, ssm_d, w_glu, b_glu, w_out,
              ln1_g, ln1_b, w_up, w_down, ln2_g, ln2_b):
    slopes = alibi_slopes(H_ATT)
    yp, ys = x_prompt, x_sample
    kp_all, vp_all, hrp_all, hip_all = [], [], [], []
    ks_all, vs_all, hrs_all, his_all = [], [], [], []
    zeros_state = jnp.zeros((x_prompt.shape[0], SSM_G, SSM_P), jnp.float32)
    for l in range(DEPTH):
        lam_init = 0.8 - 0.6 * math.exp(-0.3 * l)
        params = (w_in[l], lambda_q1[l], lambda_k1[l], lambda_q2[l], lambda_k2[l], subln_g[l],
                  ssm_a_re[l], ssm_a_im[l], ssm_log_dt[l], ssm_b_re[l], ssm_b_im[l], ssm_c_re[l],
                  ssm_c_im[l], ssm_d[l], w_glu[l], b_glu[l], w_out[l], ln1_g[l], ln1_b[l],
                  w_up[l], w_down[l], ln2_g[l], ln2_b[l])
        p_att = functools.partial(prompt_attend, slopes=slopes)
        s_att = functools.partial(sample_attend, slopes=slopes, cache_k_l=cache_k[l],
                                  cache_v_l=cache_v[l], page_table=page_table)
        yp, kp, vp, hrp, hip = hybrid_layer(yp, p_att, zeros_state, zeros_state, lam_init, *params)
        ys, ksm, vsm, hrs, his = hybrid_layer(ys, s_att, state_ssm_re[l], state_ssm_im[l], lam_init, *params)
        kp_all.append(kp); vp_all.append(vp); hrp_all.append(hrp); hip_all.append(hip)
        ks_all.append(ksm); vs_all.append(vsm); hrs_all.append(hrs); his_all.append(his)
    return (yp, ys,
            jnp.stack(kp_all), jnp.stack(vp_all), jnp.stack(hrp_all), jnp.stack(hip_all),
            jnp.stack(ks_all), jnp.stack(vs_all), jnp.stack(hrs_all), jnp.stack(his_all))
```

```python
import functools
import math

import jax
import jax.numpy as jnp
from jax import lax
from jax.experimental import pallas as pl
from jax.experimental.pallas import tpu as pltpu

F32 = jnp.float32
BF16 = jnp.bfloat16
LN_EPS = 1e-5
NEG_BIG = -1e30
SUBLANES = 8
VMEM_LIMIT_BYTES = 56 * 1024 * 1024


def _params(n_axes):
    return pltpu.CompilerParams(dimension_semantics=("arbitrary",) * n_axes,
                                vmem_limit_bytes=VMEM_LIMIT_BYTES)


def _resident(shape):
    nd = len(shape)
    return pl.BlockSpec(shape, lambda *_: (0,) * nd, pipeline_mode=pl.Buffered(1))


def _proj_in_kernel(x_ref, w_ref, q_ref, k_ref, v_ref, kb_ref, vb_ref, u_ref, *, qk_w, att_w, n_heads, q_scale):
    tm = x_ref.shape[0]
    xb = x_ref[...].astype(BF16)
    q = jnp.dot(xb, w_ref[:, :qk_w], preferred_element_type=F32)
    q_ref[...] = (q * q_scale).astype(BF16)

    def store_heads(ref, val):
        w = val.shape[1] // n_heads
        for hh in range(n_heads):
            ref[pl.ds(hh, tm, stride=n_heads), :] = val[:, hh * w:(hh + 1) * w]

    k = jnp.dot(xb, w_ref[:, qk_w:2 * qk_w], preferred_element_type=F32)
    store_heads(k_ref, k)
    kb_ref[...] = k.astype(BF16)
    v = jnp.dot(xb, w_ref[:, 2 * qk_w:2 * qk_w + att_w], preferred_element_type=F32)
    store_heads(v_ref, v)
    vb_ref[...] = v.astype(BF16)
    u_ref[...] = jnp.dot(xb, w_ref[:, 2 * qk_w + att_w:], preferred_element_type=F32)


def _proj_in(x, w_b, *, qk_w, att_w, n_heads, q_scale, tm):
    rows, d = x.shape
    ssm_w = w_b.shape[1] - 2 * qk_w - att_w
    row_block = lambda w: pl.BlockSpec((tm, w), lambda i: (i, 0))
    head_block = lambda w: pl.BlockSpec((tm * n_heads, w // n_heads), lambda i: (i, 0))
    head_shape = lambda w: jax.ShapeDtypeStruct((rows * n_heads, w // n_heads), F32)
    return pl.pallas_call(
        functools.partial(_proj_in_kernel, qk_w=qk_w, att_w=att_w, n_heads=n_heads, q_scale=q_scale),
        grid=(rows // tm,),
        in_specs=[row_block(d), _resident(w_b.shape)],
        out_specs=[row_block(qk_w), head_block(qk_w), head_block(att_w), row_block(qk_w), row_block(att_w),
                   row_block(ssm_w)],
        out_shape=[jax.ShapeDtypeStruct((rows, qk_w), BF16), head_shape(qk_w), head_shape(att_w),
                   jax.ShapeDtypeStruct((rows, qk_w), BF16), jax.ShapeDtypeStruct((rows, att_w), BF16),
                   jax.ShapeDtypeStruct((rows, ssm_w), F32)],
        compiler_params=_params(1),
        name="proj_in",
    )(x, w_b)


def _diff_lambda(lq1_ref, lk1_ref, lq2_ref, lk2_ref, lam_init):
    s1 = jnp.sum(lq1_ref[...] * lk1_ref[...], axis=-1, keepdims=True)
    s2 = jnp.sum(lq2_ref[...] * lk2_ref[...], axis=-1, keepdims=True)
    return jnp.exp(s1) - jnp.exp(s2) + lam_init


def _sub_layer_norm(o, g, lam_init):
    return o * lax.rsqrt(jnp.mean(o * o, axis=-1, keepdims=True) + LN_EPS) * g * (1.0 - lam_init)


def _prompt_attn_kernel(slopes_ref, lq1_ref, lk1_ref, lq2_ref, lk2_ref, g_ref, q_ref, k_ref, v_ref, o_ref,
                        qs_scr, m_scr, l_scr, acc_scr, *, tq, dk, lam_init):
    h = pl.program_id(1)
    qi = pl.program_id(2)
    slope = slopes_ref[h]

    q = q_ref[...].astype(F32)
    lane = lax.broadcasted_iota(jnp.int32, q.shape, 1)
    qs_scr[0:tq, :] = jnp.where(lane < dk, q, 0.0).astype(BF16)
    qs_scr[tq:2 * tq, :] = jnp.where(lane >= dk, q, 0.0).astype(BF16)
    m_scr[...] = jnp.full(m_scr.shape, NEG_BIG, F32)
    l_scr[...] = jnp.zeros(l_scr.shape, F32)
    acc_scr[...] = jnp.zeros(acc_scr.shape, F32)

    def block(ki, masked):
        start = pl.multiple_of(ki * tq, tq)
        kb = k_ref[pl.ds(start, tq), :]
        vb = v_ref[pl.ds(start, tq), :]
        s = lax.dot_general(qs_scr[...], kb, (((1,), (1,)), ((), ())), preferred_element_type=F32)
        jj = lax.broadcasted_iota(jnp.int32, (1, tq), 1)
        s = s + slope * ((ki - qi) * tq + jj).astype(F32)
        if masked:
            row = lax.broadcasted_iota(jnp.int32, s.shape, 0) & (tq - 1)
            col = lax.broadcasted_iota(jnp.int32, s.shape, 1)
            s = jnp.where(col <= row, s, NEG_BIG)
        m_prev = m_scr[...]
        m_new = jnp.maximum(m_prev, jnp.max(s, axis=1, keepdims=True))
        alpha = jnp.exp(m_prev - m_new)
        p = jnp.exp(s - m_new)
        l_scr[...] = alpha * l_scr[...] + jnp.sum(p, axis=1, keepdims=True)
        acc_scr[...] = alpha * acc_scr[...] + jnp.dot(p.astype(BF16), vb, preferred_element_type=F32)
        m_scr[...] = m_new

    def body(ki, carry):
        block(ki, False)
        return carry

    lax.fori_loop(0, qi, body, 0)
    block(qi, True)

    o_maps = acc_scr[...] / l_scr[...]
    lam = _diff_lambda(lq1_ref, lk1_ref, lq2_ref, lk2_ref, lam_init)
    o = o_maps[:tq] - lam * o_maps[tq:]
    o_ref[...] = _sub_layer_norm(o, g_ref[...], lam_init).astype(o_ref.dtype)


def _prompt_attention(q, kb, vb, slopes, lam_vecs, subln_g, *, n, t, h, dk, dv, lam_init, tq):
    assert tq & (tq - 1) == 0 and t % tq == 0 and dv == 2 * dk
    nq = t // tq
    vec = lambda w: pl.BlockSpec((1, w), lambda b, hh, i, s: (0, 0))
    grid_spec = pltpu.PrefetchScalarGridSpec(
        num_scalar_prefetch=1,
        grid=(n, h, nq),
        in_specs=[vec(dk), vec(dk), vec(dk), vec(dk), vec(dv),
                  pl.BlockSpec((tq, dv), lambda b, hh, i, s: (b * nq + i, hh)),
                  pl.BlockSpec((t, dv), lambda b, hh, i, s: (b, hh)),
                  pl.BlockSpec((t, dv), lambda b, hh, i, s: (b, hh))],
        out_specs=pl.BlockSpec((tq, dv), lambda b, hh, i, s: (b * nq + i, hh)),
        scratch_shapes=[pltpu.VMEM((2 * tq, dv), BF16), pltpu.VMEM((2 * tq, 1), F32),
                        pltpu.VMEM((2 * tq, 1), F32), pltpu.VMEM((2 * tq, dv), F32)],
    )
    return pl.pallas_call(
        functools.partial(_prompt_attn_kernel, tq=tq, dk=dk, lam_init=lam_init),
        grid_spec=grid_spec,
        out_shape=jax.ShapeDtypeStruct((n * t, h * dv), BF16),
        compiler_params=_params(3),
        name="prompt_attention",
    )(slopes, *lam_vecs, subln_g, q, kb, vb)


def _sample_attn_kernel(pt_ref, slopes_ref, lq1_ref, lk1_ref, lq2_ref, lk2_ref, g_ref, q_ref, kn_ref, vn_ref,
                        *rest, n_heads, dk, tnew, page, pages_per_step, past, lam_init):
    del pt_ref
    k_refs = rest[:pages_per_step]
    v_refs = rest[pages_per_step:2 * pages_per_step]
    o_ref = rest[2 * pages_per_step]
    qx_scr, m_scr, l_scr, acc_scr = rest[2 * pages_per_step + 1:]
    j = pl.program_id(1)
    dv = 2 * dk
    hrows = 2 * tnew
    rows = n_heads * hrows
    row_i = lax.broadcasted_iota(jnp.int32, (rows, 1), 0)
    slope_rows = jnp.zeros((rows, 1), F32)
    for hh in range(n_heads):
        slope_rows = jnp.where(row_i // hrows == hh, slopes_ref[hh], slope_rows)
    nt = (((1,), (1,)), ((), ()))

    def head_rows(ref, hh):
        return ref[pl.ds(hh, page, stride=n_heads), :].astype(BF16)

    @pl.when(j == 0)
    def _():
        lane = lax.broadcasted_iota(jnp.int32, (tnew, dv), 1)
        pieces = []
        for hh in range(n_heads):
            qh = q_ref[:, hh * dv:(hh + 1) * dv]
            pieces += [jnp.where(lane < dk, qh, 0.0), jnp.where(lane >= dk, qh, 0.0)]
        qx_scr[...] = jnp.concatenate(pieces, axis=0).astype(BF16)
        m_scr[...] = jnp.full(m_scr.shape, NEG_BIG, F32)
        l_scr[...] = jnp.zeros(l_scr.shape, F32)
        acc_scr[...] = jnp.zeros(acc_scr.shape, F32)

    def scores(k_of):
        out = []
        for hh in range(n_heads):
            qh = qx_scr[hh * hrows:(hh + 1) * hrows, :]
            out.append(jnp.concatenate(
                [lax.dot_general(qh, kc, nt, preferred_element_type=F32) for kc in k_of(hh)], axis=1))
        return jnp.concatenate(out, axis=0)

    def update(s, v_of):
        m_prev = m_scr[...]
        m_new = jnp.maximum(m_prev, jnp.max(s, axis=1, keepdims=True))
        alpha = jnp.exp(m_prev - m_new)
        p = jnp.exp(s - m_new)
        l_scr[...] = alpha * l_scr[...] + jnp.sum(p, axis=1, keepdims=True)
        pb = p.astype(BF16)
        pv_rows = []
        for hh in range(n_heads):
            pv, off = None, 0
            for vc in v_of(hh):
                part = jnp.dot(pb[hh * hrows:(hh + 1) * hrows, off:off + vc.shape[0]], vc,
                               preferred_element_type=F32)
                pv = part if pv is None else pv + part
                off += vc.shape[0]
            pv_rows.append(pv)
        acc_scr[...] = alpha * acc_scr[...] + jnp.concatenate(pv_rows, axis=0)
        m_scr[...] = m_new

    s = scores(lambda hh: [head_rows(k_refs[i], hh) for i in range(pages_per_step)])
    kpos = j * (pages_per_step * page) + lax.broadcasted_iota(jnp.int32, (1, s.shape[1]), 1)
    s = s + slope_rows * (kpos - past).astype(F32)
    update(s, lambda hh: [head_rows(v_refs[i], hh) for i in range(pages_per_step)])

    @pl.when(j == pl.num_programs(1) - 1)
    def _():
        pad = jnp.zeros((page - tnew, dv), F32)

        def new_rows(ref, hh):
            return [jnp.concatenate([ref[pl.ds(hh, tnew, stride=n_heads), :], pad], axis=0).astype(BF16)]

        sn = scores(lambda hh: new_rows(kn_ref, hh))
        tcol = lax.broadcasted_iota(jnp.int32, sn.shape, 1)
        trow = lax.broadcasted_iota(jnp.int32, sn.shape, 0) % tnew
        sn = sn + slope_rows * tcol.astype(F32)
        sn = jnp.where(tcol <= trow, sn, NEG_BIG)
        update(sn, lambda hh: new_rows(vn_ref, hh))
        o_all = acc_scr[...] / l_scr[...]
        lam = _diff_lambda(lq1_ref, lk1_ref, lq2_ref, lk2_ref, lam_init)
        g = g_ref[...]
        for hh in range(n_heads):
            blk = o_all[hh * hrows:(hh + 1) * hrows]
            o = blk[:tnew] - lam * blk[tnew:]
            o_ref[:, hh * dv:(hh + 1) * dv] = _sub_layer_norm(o, g, lam_init).astype(o_ref.dtype)


def _sample_attention(q, kn, vn, cache_k, cache_v, page_table, slopes, lam_vecs, subln_g, *,
                      n_heads, dk, tnew, lam_init, pages_per_step):
    n_seq, n_pages = page_table.shape
    dv = 2 * dk
    width = n_heads * dv
    page = cache_k.shape[1] // n_heads
    assert n_pages % pages_per_step == 0 and tnew % SUBLANES == 0 and tnew <= page and cache_k.shape[2] == dv
    rows = n_heads * 2 * tnew
    vec = lambda w: pl.BlockSpec((1, w), lambda b, j, pt, s: (0, 0))
    seq_rows = pl.BlockSpec((tnew, width), lambda b, j, pt, s: (b, 0))
    seq_heads = pl.BlockSpec((tnew * n_heads, dv), lambda b, j, pt, s: (b, 0))

    def page_spec(i):
        return pl.BlockSpec((None, page * n_heads, dv),
                            lambda b, j, pt, s: (pt[b, j * pages_per_step + i], 0, 0))

    grid_spec = pltpu.PrefetchScalarGridSpec(
        num_scalar_prefetch=2,
        grid=(n_seq, n_pages // pages_per_step),
        in_specs=[vec(dk), vec(dk), vec(dk), vec(dk), vec(dv), seq_rows, seq_heads, seq_heads]
        + [page_spec(i) for i in range(pages_per_step)] * 2,
        out_specs=pl.BlockSpec((tnew, width), lambda b, j, pt, s: (b, 0)),
        scratch_shapes=[pltpu.VMEM((rows, dv), BF16), pltpu.VMEM((rows, 1), F32),
                        pltpu.VMEM((rows, 1), F32), pltpu.VMEM((rows, dv), F32)],
    )
    return pl.pallas_call(
        functools.partial(_sample_attn_kernel, n_heads=n_heads, dk=dk, tnew=tnew, page=page,
                          pages_per_step=pages_per_step, past=n_pages * page, lam_init=lam_init),
        grid_spec=grid_spec,
        out_shape=jax.ShapeDtypeStruct((n_seq * tnew, width), F32),
        compiler_params=_params(2),
        name="sample_attention",
    )(page_table, slopes, *lam_vecs, subln_g, q, kn, vn,
      *([cache_k] * pages_per_step), *([cache_v] * pages_per_step))


def _s5_prep_kernel(are_ref, aim_ref, ldt_ref, bre_ref, bim_ref, pwre_ref, pwim_ref, bbre_ref, bbim_ref,
                    *, seg, n_blocks):
    lam_re = jnp.minimum(are_ref[...], -1e-4)
    lam_im = aim_ref[...]
    dt = jnp.exp(ldt_ref[...])
    z_re, z_im = lam_re * dt, lam_im * dt
    ez = jnp.exp(z_re)
    ab_re, ab_im = ez * jnp.cos(z_im), ez * jnp.sin(z_im)
    den = lam_re * lam_re + lam_im * lam_im
    nr, ni = ab_re - 1.0, ab_im
    f_re = (nr * lam_re + ni * lam_im) / den
    f_im = (ni * lam_re - nr * lam_im) / den
    steps = (lax.broadcasted_iota(jnp.int32, (seg, 1), 0) + 1).astype(F32)
    ek = jnp.exp(z_re * steps)
    pwre_ref[...] = ek * jnp.cos(z_im * steps)
    pwim_ref[...] = ek * jnp.sin(z_im * steps)
    w = bbre_ref.shape[2]
    for jb in range(n_blocks):
        fr, fi = f_re[:, jb * w:(jb + 1) * w], f_im[:, jb * w:(jb + 1) * w]
        b_re, b_im = bre_ref[jb], bim_ref[jb]
        bbre_ref[jb] = (fr * b_re - fi * b_im).astype(BF16)
        bbim_ref[jb] = (fr * b_im + fi * b_re).astype(BF16)


def _s5_prep(a_re_row, a_im_row, log_dt_row, b_re_bd, b_im_bd, *, seg):
    n_blocks, kb, wb = b_re_bd.shape
    gp = a_re_row.shape[1]
    return pl.pallas_call(
        functools.partial(_s5_prep_kernel, seg=seg, n_blocks=n_blocks),
        out_shape=[jax.ShapeDtypeStruct((seg, gp), F32), jax.ShapeDtypeStruct((seg, gp), F32),
                   jax.ShapeDtypeStruct((n_blocks, kb, wb), BF16), jax.ShapeDtypeStruct((n_blocks, kb, wb), BF16)],
        compiler_params=pltpu.CompilerParams(vmem_limit_bytes=VMEM_LIMIT_BYTES),
        name="s5_prep",
    )(a_re_row, a_im_row, log_dt_row, b_re_bd, b_im_bd)


def _gelu_tanh(y):
    return 0.5 * y * (1.0 + jnp.tanh(math.sqrt(2.0 / math.pi) * (y + 0.044715 * (y * y * y))))


def _s5_kernel(u_ref, h0re_ref, h0im_ref, pwre_ref, pwim_ref, bbre_ref, bbim_ref, cre_ref, cim_ref, d_ref,
               wglu_ref, bglu_ref, s_ref, hre_ref, him_ref, xre_scr, xim_scr, y_scr, cre_scr, cim_scr,
               *, seg, chained):
    rows = SUBLANES * seg
    n_blocks, kb, wb = bbre_ref.shape
    cb = cre_ref.shape[2]

    ri = lax.broadcasted_iota(jnp.int32, (rows, rows), 0)
    ci = lax.broadcasted_iota(jnp.int32, (rows, rows), 1)
    perm = jnp.where(ci == (ri % SUBLANES) * seg + ri // SUBLANES, 1.0, 0.0).astype(BF16)
    perm_t = jnp.where(ri == (ci % SUBLANES) * seg + ci // SUBLANES, 1.0, 0.0).astype(BF16)

    u = u_ref[...]
    u_hi = u.astype(BF16)
    u_lo = (u - u_hi.astype(F32)).astype(BF16)
    up_hi = jnp.dot(perm, u_hi, preferred_element_type=F32)
    up = up_hi + jnp.dot(perm, u_lo, preferred_element_type=F32)
    up_b = up_hi.astype(BF16)

    if chained:
        @pl.when(pl.program_id(1) == 0)
        def _():
            cre_scr[...] = h0re_ref[0]
            cim_scr[...] = h0im_ref[0]

    for jb in range(n_blocks):
        lanes = slice(jb * wb, (jb + 1) * wb)
        uj = up_b[:, jb * kb:(jb + 1) * kb]
        xre_scr[...] = jnp.dot(uj, bbre_ref[jb], preferred_element_type=F32).reshape(seg, SUBLANES, wb)
        xim_scr[...] = jnp.dot(uj, bbim_ref[jb], preferred_element_type=F32).reshape(seg, SUBLANES, wb)
        a_re = jnp.broadcast_to(pwre_ref[0:1, lanes], (SUBLANES, wb))
        a_im = jnp.broadcast_to(pwim_ref[0:1, lanes], (SUBLANES, wb))

        def scan_step(t, state):
            xr, xi = state
            nr = a_re * xr - a_im * xi + xre_scr[t]
            ni = a_re * xi + a_im * xr + xim_scr[t]
            xre_scr[t] = nr
            xim_scr[t] = ni
            return nr, ni

        zeros = jnp.zeros((SUBLANES, wb), F32)
        end_re, end_im = lax.fori_loop(0, seg, scan_step, (zeros, zeros), unroll=4)

        if chained:
            as_re, as_im = pwre_ref[seg - 1:seg, lanes], pwim_ref[seg - 1:seg, lanes]
            c_re, c_im = cre_scr[:, lanes], cim_scr[:, lanes]
            seg_row = lax.broadcasted_iota(jnp.int32, (SUBLANES, wb), 0)
            in_re = jnp.broadcast_to(c_re, (SUBLANES, wb))
            in_im = jnp.broadcast_to(c_im, (SUBLANES, wb))
            for r in range(1, SUBLANES + 1):
                e_re, e_im = end_re[r - 1:r], end_im[r - 1:r]
                c_re, c_im = e_re + as_re * c_re - as_im * c_im, e_im + as_re * c_im + as_im * c_re
                if r < SUBLANES:
                    in_re = jnp.where(seg_row == r, c_re, in_re)
                    in_im = jnp.where(seg_row == r, c_im, in_im)
            cre_scr[:, lanes] = c_re
            cim_scr[:, lanes] = c_im
            hre_ref[0, :, lanes] = c_re
            him_ref[0, :, lanes] = c_im
        else:
            in_re, in_im = h0re_ref[:, lanes], h0im_ref[:, lanes]

        def fix_step(t, carry):
            p_re = pwre_ref[pl.ds(t, 1), lanes]
            p_im = pwim_ref[pl.ds(t, 1), lanes]
            xre_scr[t] = xre_scr[t] + p_re * in_re - p_im * in_im
            xim_scr[t] = xim_scr[t] + p_re * in_im + p_im * in_re
            return carry

        lax.fori_loop(0, seg, fix_step, 0, unroll=4)
        if not chained:
            hre_ref[:, lanes] = xre_scr[seg - 1]
            him_ref[:, lanes] = xim_scr[seg - 1]
        x_re = xre_scr[...].reshape(rows, wb).astype(BF16)
        x_im = xim_scr[...].reshape(rows, wb).astype(BF16)
        y_scr[:, jb * cb:(jb + 1) * cb] = (jnp.dot(x_re, cre_ref[jb], preferred_element_type=F32)
                                           - jnp.dot(x_im, cim_ref[jb], preferred_element_type=F32))

    y = y_scr[...] + d_ref[...] * up
    g = _gelu_tanh(y)
    gate = jax.nn.sigmoid(jnp.dot(g.astype(BF16), wglu_ref[...], preferred_element_type=F32) + bglu_ref[...])
    s_perm = (g * gate).astype(BF16)
    s_ref[...] = jnp.dot(perm_t, s_perm, preferred_element_type=F32).astype(s_ref.dtype)


def _s5(u, h0_re, h0_im, pw_re, pw_im, bb_re, bb_im, c_re_bd, c_im_bd, d_row, w_glu_b, b_glu_row,
        *, n_seq, t, seg, chained):
    rows = SUBLANES * seg
    ssm_w = u.shape[1]
    gp = pw_re.shape[1]
    n_blocks, kb, wb = bb_re.shape
    pw = pw_re[:seg], pw_im[:seg]
    consts = [_resident((seg, gp)), _resident((seg, gp)), _resident(bb_re.shape), _resident(bb_im.shape),
              _resident(c_re_bd.shape), _resident(c_im_bd.shape), _resident(d_row.shape),
              _resident(w_glu_b.shape), _resident(b_glu_row.shape)]
    scratch = [pltpu.VMEM((seg, SUBLANES, wb), F32), pltpu.VMEM((seg, SUBLANES, wb), F32),
               pltpu.VMEM((rows, ssm_w), F32), pltpu.VMEM((1, gp), F32), pltpu.VMEM((1, gp), F32)]
    if chained:
        assert t % rows == 0
        nc = t // rows
        grid = (n_seq, nc)
        u_spec = pl.BlockSpec((rows, ssm_w), lambda b, c: (b * nc + c, 0))
        h_spec = pl.BlockSpec((1, 1, gp), lambda b, c: (b, 0, 0))
        h_shape = jax.ShapeDtypeStruct((n_seq, 1, gp), F32)
    else:
        assert t == seg and n_seq % SUBLANES == 0
        grid = (n_seq // SUBLANES,)
        u_spec = pl.BlockSpec((rows, ssm_w), lambda b: (b, 0))
        h_spec = pl.BlockSpec((SUBLANES, gp), lambda b: (b, 0))
        h_shape = jax.ShapeDtypeStruct((n_seq, gp), F32)
    return pl.pallas_call(
        functools.partial(_s5_kernel, seg=seg, chained=chained),
        grid=grid,
        in_specs=[u_spec, h_spec, h_spec] + consts,
        out_specs=[u_spec, h_spec, h_spec],
        out_shape=[jax.ShapeDtypeStruct(u.shape, BF16), h_shape, h_shape],
        scratch_shapes=scratch,
        compiler_params=_params(len(grid)),
        name="s5_chained" if chained else "s5_batched",
    )(u, h0_re, h0_im, *pw, bb_re, bb_im, c_re_bd, c_im_bd, d_row, w_glu_b, b_glu_row)


def _layer_norm(x, g, b):
    mu = jnp.mean(x, axis=-1, keepdims=True)
    xc = x - mu
    var = jnp.mean(xc * xc, axis=-1, keepdims=True)
    return xc * lax.rsqrt(var + LN_EPS) * g + b


def _out_proj_kernel(x_ref, o_ref, s_ref, wo_ref, ws_ref, g_ref, b_ref, x1_ref, x1b_ref, *, alpha):
    mix = (jnp.dot(o_ref[...].astype(BF16), wo_ref[...], preferred_element_type=F32)
           + jnp.dot(s_ref[...].astype(BF16), ws_ref[...], preferred_element_type=F32))
    x1 = _layer_norm(alpha * x_ref[...] + mix, g_ref[...], b_ref[...])
    x1_ref[...] = x1
    x1b_ref[...] = x1.astype(BF16)


def _out_proj(x, o, s, w_o, w_s, g_row, b_row, *, alpha, tm):
    rows, d = x.shape
    row_block = lambda w: pl.BlockSpec((tm, w), lambda i: (i, 0))
    return pl.pallas_call(
        functools.partial(_out_proj_kernel, alpha=alpha),
        grid=(rows // tm,),
        in_specs=[row_block(d), row_block(o.shape[1]), row_block(s.shape[1]), _resident(w_o.shape),
                  _resident(w_s.shape), _resident(g_row.shape), _resident(b_row.shape)],
        out_specs=[row_block(d), row_block(d)],
        out_shape=[jax.ShapeDtypeStruct((rows, d), F32), jax.ShapeDtypeStruct((rows, d), BF16)],
        compiler_params=_params(1),
        name="out_proj_ln",
    )(x, o, s, w_o, w_s, g_row, b_row)


def _ffn_kernel(x1_ref, x1b_ref, wup_ref, wdn_ref, g_ref, b_ref, x2_ref, acc_scr, *, alpha):
    f = pl.program_id(1)

    @pl.when(f == 0)
    def _():
        acc_scr[...] = jnp.zeros(acc_scr.shape, F32)

    hidden = jnp.maximum(jnp.dot(x1b_ref[...], wup_ref[...], preferred_element_type=F32), 0.0)
    hidden = (hidden * hidden).astype(BF16)
    acc_scr[...] += jnp.dot(hidden, wdn_ref[...], preferred_element_type=F32)

    @pl.when(f == pl.num_programs(1) - 1)
    def _():
        x2_ref[...] = _layer_norm(alpha * x1_ref[...] + acc_scr[...], g_ref[...], b_ref[...])


def _ffn(x1, x1b, w_up, w_dn, g_row, b_row, *, alpha, tm, tf):
    rows, d = x1.shape
    dff = w_up.shape[1]
    return pl.pallas_call(
        functools.partial(_ffn_kernel, alpha=alpha),
        grid=(rows // tm, dff // tf),
        in_specs=[pl.BlockSpec((tm, d), lambda i, f: (i, 0)), pl.BlockSpec((tm, d), lambda i, f: (i, 0)),
                  pl.BlockSpec((d, tf), lambda i, f: (0, f)), pl.BlockSpec((tf, d), lambda i, f: (f, 0)),
                  _resident(g_row.shape), _resident(b_row.shape)],
        out_specs=pl.BlockSpec((tm, d), lambda i, f: (i, 0)),
        out_shape=jax.ShapeDtypeStruct((rows, d), F32),
        scratch_shapes=[pltpu.VMEM((tm, d), F32)],
        compiler_params=_params(2),
        name="ffn_ln",
    )(x1, x1b, w_up, w_dn, g_row, b_row)


def _block_diag(x, groups_per_block):
    g, r, c = x.shape
    nb = g // groups_per_block
    eye = jnp.eye(groups_per_block, dtype=x.dtype)
    x = x.reshape(nb, groups_per_block, r, c)
    return jnp.einsum("jlrc,lm->jlrmc", x, eye).reshape(nb, groups_per_block * r, groups_per_block * c)


def _pick_tile(rows, target):
    t = min(rows, target)
    while rows % t:
        t //= 2
    return t


def _layer(x, attend, s5_run, prm, *, alpha, dims):
    tm = _pick_tile(x.shape[0], 512)
    q, k, v, kb, vb, u = _proj_in(x, prm["w_in"], qk_w=dims["qk_w"], att_w=dims["att_w"], n_heads=dims["n_heads"],
                                  q_scale=dims["dk"] ** -0.5, tm=tm)
    o = attend(q, k, v, kb, vb)
    s, h_re, h_im = s5_run(u)
    x1, x1b = _out_proj(x, o, s, prm["w_out_att"], prm["w_out_ssm"], prm["ln1_g"], prm["ln1_b"], alpha=alpha, tm=tm)
    x2 = _ffn(x1, x1b, prm["w_up"], prm["w_down"], prm["ln2_g"], prm["ln2_b"], alpha=alpha, tm=tm,
              tf=_pick_tile(prm["w_up"].shape[1], 1024))
    return x2, k, v, h_re, h_im


def kernel(x_prompt, x_sample, cache_k, cache_v, state_ssm_re, state_ssm_im, page_table, w_in, lambda_q1, lambda_k1, lambda_q2, lambda_k2, subln_g, ssm_a_re, ssm_a_im, ssm_log_dt, ssm_b_re, ssm_b_im, ssm_c_re, ssm_c_im, ssm_d, w_glu, b_glu, w_out, ln1_g, ln1_b, w_up, w_down, ln2_g, ln2_b):
    depth, d_model, _ = w_in.shape
    n_p, t_p, _ = x_prompt.shape
    n_s, t_s, _ = x_sample.shape
    _, n_pool, page, n_heads, two_dk = cache_k.shape
    dk = two_dk // 2
    dv = cache_v.shape[-1]
    qk_w, att_w = n_heads * two_dk, n_heads * dv
    ssm_w = w_in.shape[2] - 2 * qk_w - att_w
    n_groups, n_states, n_chan = ssm_b_re.shape[1:]
    gp = n_groups * n_states
    gpb = 16
    assert math.log2(dk ** -0.5).is_integer() and dv == two_dk and ssm_w == n_groups * n_chan
    dims = dict(qk_w=qk_w, att_w=att_w, dk=dk, n_heads=n_heads)
    alpha = (2.0 * depth) ** 0.25
    slopes = jnp.exp2(-8.0 * jnp.arange(1, n_heads + 1, dtype=F32) / n_heads)
    seg_p = 32
    assert t_s <= seg_p

    yp = x_prompt.reshape(n_p * t_p, d_model)
    ys = x_sample.reshape(n_s * t_s, d_model)
    zeros_state = jnp.zeros((n_p, 1, gp), F32)
    outs = [[] for _ in range(8)]
    for l in range(depth):
        lam_init = 0.8 - 0.6 * math.exp(-0.3 * l)
        row = lambda a: a[l].reshape(1, -1)
        prm = dict(
            w_in=w_in[l].astype(BF16), w_out_att=w_out[l, :att_w].astype(BF16),
            w_out_ssm=w_out[l, att_w:].astype(BF16), w_up=w_up[l].astype(BF16), w_down=w_down[l].astype(BF16),
            ln1_g=row(ln1_g), ln1_b=row(ln1_b), ln2_g=row(ln2_g), ln2_b=row(ln2_b))
        lam_vecs = (row(lambda_q1), row(lambda_k1), row(lambda_q2), row(lambda_k2))
        g_row = row(subln_g)

        pw_re, pw_im, bb_re, bb_im = _s5_prep(
            row(ssm_a_re), row(ssm_a_im), jnp.repeat(ssm_log_dt[l], n_states).reshape(1, gp),
            _block_diag(jnp.swapaxes(ssm_b_re[l], 1, 2), gpb), _block_diag(jnp.swapaxes(ssm_b_im[l], 1, 2), gpb),
            seg=seg_p)
        s5_consts = (pw_re, pw_im, bb_re, bb_im,
                     _block_diag(jnp.swapaxes(ssm_c_re[l], 1, 2), gpb).astype(BF16),
                     _block_diag(jnp.swapaxes(ssm_c_im[l], 1, 2), gpb).astype(BF16),
                     row(ssm_d), w_glu[l].astype(BF16), row(b_glu))

        p_att = lambda q, k, v, kb, vb: _prompt_attention(
            q, kb, vb, slopes, lam_vecs, g_row, n=n_p, t=t_p, h=n_heads, dk=dk, dv=dv, lam_init=lam_init,
            tq=_pick_tile(t_p, 512))
        p_s5 = lambda u: _s5(u, zeros_state, zeros_state, *s5_consts, n_seq=n_p, t=t_p, seg=seg_p, chained=True)
        yp, kp, vp, hrp, hip = _layer(yp, p_att, p_s5, prm, alpha=alpha, dims=dims)

        ck = cache_k[l].reshape(n_pool, page * n_heads, two_dk)
        cv = cache_v[l].reshape(n_pool, page * n_heads, dv)
        s_att = lambda q, k, v, kb, vb: _sample_attention(
            q.astype(F32), k, v, ck, cv, page_table, slopes, lam_vecs, g_row, n_heads=n_heads, dk=dk, tnew=t_s,
            lam_init=lam_init, pages_per_step=8)
        s_s5 = lambda u: _s5(u, state_ssm_re[l].reshape(n_s, gp), state_ssm_im[l].reshape(n_s, gp), *s5_consts,
                             n_seq=n_s, t=t_s, seg=t_s, chained=False)
        ys, ksm, vsm, hrs, his = _layer(ys, s_att, s_s5, prm, alpha=alpha, dims=dims)

        for acc, val in zip(outs, (
                kp.reshape(n_p, t_p, n_heads, two_dk), vp.reshape(n_p, t_p, n_heads, dv),
                hrp.reshape(n_p, n_groups, n_states), hip.reshape(n_p, n_groups, n_states),
                ksm.reshape(n_s, t_s, n_heads, two_dk), vsm.reshape(n_s, t_s, n_heads, dv),
                hrs.reshape(n_s, n_groups, n_states), his.reshape(n_s, n_groups, n_states))):
            acc.append(val)
    return (yp.reshape(x_prompt.shape), ys.reshape(x_sample.shape), *(jnp.stack(o) for o in outs))
```

```python
import functools
import math

import jax
import jax.numpy as jnp
from jax import lax
from jax.experimental import pallas as pl
from jax.experimental.pallas import tpu as pltpu

F32 = jnp.float32
BF16 = jnp.bfloat16
LN_EPS = 1e-5
NEG_BIG = -1e30
LANES = 128
SUBLANES = 8
BF16_ROWS = 16
POS_SPLIT = 64
POS_PIECES = 3
VMEM_LIMIT_BYTES = 56 * 1024 * 1024


def _params(n_axes):
    return pltpu.CompilerParams(dimension_semantics=("arbitrary",) * n_axes,
                                vmem_limit_bytes=VMEM_LIMIT_BYTES)


def _resident(shape):
    nd = len(shape)
    return pl.BlockSpec(shape, lambda *_: (0,) * nd, pipeline_mode=pl.Buffered(1))


def _proj_in_kernel(x_ref, w_ref, *out_refs, qk_w, att_w, n_heads, q_scale, for_flash, seq_len):
    if for_flash:
        q_ref, k_ref, v_ref, kb_ref, vt_ref, u_ref = out_refs
    else:
        q_ref, k_ref, v_ref, u_ref = out_refs
    tm = x_ref.shape[0]
    xb = x_ref[...].astype(BF16)
    q = jnp.dot(xb, w_ref[:, :qk_w], preferred_element_type=F32) * q_scale
    if for_flash:
        q_ref[...] = q.T.astype(BF16)
    else:
        q_ref[...] = q.astype(BF16).astype(F32)

    def store_heads(ref, val):
        w = val.shape[1] // n_heads
        for hh in range(n_heads):
            ref[pl.ds(hh, tm, stride=n_heads), :] = val[:, hh * w:(hh + 1) * w]

    k = jnp.dot(xb, w_ref[:, qk_w:2 * qk_w], preferred_element_type=F32)
    store_heads(k_ref, k)
    v = jnp.dot(xb, w_ref[:, 2 * qk_w:2 * qk_w + att_w], preferred_element_type=F32)
    store_heads(v_ref, v)
    if for_flash:
        w = qk_w // n_heads
        pos = lax.rem(pl.program_id(0) * tm, seq_len) + lax.broadcasted_iota(jnp.int32, (tm, w), 0)
        col = lax.broadcasted_iota(jnp.int32, (tm, w), 1)
        feat = jnp.where(col < POS_PIECES, pos >> int(math.log2(POS_SPLIT)),
                         jnp.where(col < 2 * POS_PIECES, pos & (POS_SPLIT - 1), 0)).astype(F32)
        pieces = []
        for hh in range(n_heads):
            pieces += [k[:, hh * w:(hh + 1) * w], feat]
        kb_ref[...] = jnp.concatenate(pieces, axis=1).astype(BF16)
        vt_ref[...] = v.T.astype(BF16)
    u_ref[...] = jnp.dot(xb, w_ref[:, 2 * qk_w + att_w:], preferred_element_type=F32)


def _proj_in(x, w_b, *, qk_w, att_w, n_heads, q_scale, tm, for_flash, seq_len):
    assert not for_flash or (seq_len <= POS_SPLIT * 256 and seq_len % tm == 0)
    rows, d = x.shape
    ssm_w = w_b.shape[1] - 2 * qk_w - att_w
    row_block = lambda w: pl.BlockSpec((tm, w), lambda i: (i, 0))
    head_block = lambda w: pl.BlockSpec((tm * n_heads, w // n_heads), lambda i: (i, 0))
    head_shape = lambda w: jax.ShapeDtypeStruct((rows * n_heads, w // n_heads), F32)
    t_block = lambda w: pl.BlockSpec((None, w, tm), lambda i: (i, 0, 0))
    t_shape = lambda w: jax.ShapeDtypeStruct((rows // tm, w, tm), BF16)
    if for_flash:
        out_specs = [t_block(qk_w), head_block(qk_w), head_block(att_w), row_block(2 * qk_w), t_block(att_w),
                     row_block(ssm_w)]
        out_shape = [t_shape(qk_w), head_shape(qk_w), head_shape(att_w),
                     jax.ShapeDtypeStruct((rows, 2 * qk_w), BF16), t_shape(att_w),
                     jax.ShapeDtypeStruct((rows, ssm_w), F32)]
    else:
        out_specs = [row_block(qk_w), head_block(qk_w), head_block(att_w), row_block(ssm_w)]
        out_shape = [jax.ShapeDtypeStruct((rows, qk_w), F32), head_shape(qk_w), head_shape(att_w),
                     jax.ShapeDtypeStruct((rows, ssm_w), F32)]
    return pl.pallas_call(
        functools.partial(_proj_in_kernel, qk_w=qk_w, att_w=att_w, n_heads=n_heads, q_scale=q_scale,
                          for_flash=for_flash, seq_len=seq_len),
        grid=(rows // tm,),
        in_specs=[row_block(d), _resident(w_b.shape)],
        out_specs=out_specs,
        out_shape=out_shape,
        compiler_params=_params(1),
        name="proj_in",
    )(x, w_b)


def _diff_lambda(lq1_ref, lk1_ref, lq2_ref, lk2_ref, lam_init):
    s1 = jnp.sum(lq1_ref[...] * lk1_ref[...], axis=-1, keepdims=True)
    s2 = jnp.sum(lq2_ref[...] * lk2_ref[...], axis=-1, keepdims=True)
    return jnp.exp(s1) - jnp.exp(s2) + lam_init


def _sub_layer_norm(o, g, lam_init):
    return o * lax.rsqrt(jnp.mean(o * o, axis=-1, keepdims=True) + LN_EPS) * g * (1.0 - lam_init)


def _prompt_attn_kernel(slopes_ref, lq1_ref, lk1_ref, lq2_ref, lk2_ref, gcol_ref, qt_ref, k_ref, vt_ref, o_ref,
                        qs_scr, m_scr, acc_scr, sa_scr, sb_scr, *, tq, dk, lam_init):
    h = pl.program_id(1)
    qi = pl.program_id(2)
    dv = 2 * dk

    qt = qt_ref[...].astype(F32)
    feat = lax.broadcasted_iota(jnp.int32, qt.shape, 0)
    qs_scr[0:dv, 0:tq] = jnp.where(feat < dk, qt, 0.0).astype(BF16)
    qs_scr[0:dv, tq:2 * tq] = jnp.where(feat >= dk, qt, 0.0).astype(BF16)

    @pl.when(qi == 0)
    def _():
        c = jnp.full((BF16_ROWS, 2 * tq), slopes_ref[h] * math.log2(math.e), F32)
        row = lax.broadcasted_iota(jnp.int32, c.shape, 0)
        idx = jnp.where(row < POS_PIECES, row, row - POS_PIECES)
        piece, rest = jnp.zeros_like(c), c
        for i in range(POS_PIECES):
            part = rest.astype(BF16).astype(F32)
            piece = jnp.where(idx == i, part, piece)
            rest = rest - part
        coef = jnp.where(row < POS_PIECES, POS_SPLIT * piece, jnp.where(row < 2 * POS_PIECES, piece, 0.0))
        qs_scr[dv:dv + BF16_ROWS, :] = coef.astype(BF16)
        qs_scr[dv + BF16_ROWS:, :] = jnp.zeros((dv - BF16_ROWS, 2 * tq), BF16)

    m_scr[...] = jnp.full(m_scr.shape, NEG_BIG, F32)
    acc_scr[...] = jnp.zeros(acc_scr.shape, F32)
    ones_rows = jnp.ones((BF16_ROWS, tq), BF16)

    def scores(ki, st_ref):
        start = pl.multiple_of(ki * tq, tq)
        st_ref[...] = jnp.dot(k_ref[pl.ds(start, tq), :], qs_scr[...], preferred_element_type=F32)

    def softmax_pv(st_ref, ki, masked):
        st = st_ref[...]
        if masked:
            key = lax.broadcasted_iota(jnp.int32, st.shape, 0)
            qry = lax.broadcasted_iota(jnp.int32, st.shape, 1) & (tq - 1)
            st = jnp.where(key <= qry, st, NEG_BIG)
        m_prev = m_scr[...]
        m_new = jnp.maximum(m_prev, jnp.max(st, axis=0, keepdims=True))
        alpha = jnp.exp2(m_prev - m_new)
        p = jnp.exp2(st - m_new).astype(BF16)
        vt_ones = jnp.concatenate([vt_ref[ki], ones_rows], axis=0)
        acc_scr[...] = alpha * acc_scr[...] + jnp.dot(vt_ones, p, preferred_element_type=F32)
        m_scr[...] = m_new

    scores(0, sa_scr)

    def pair(i2, carry):
        k0 = 2 * i2
        scores(k0 + 1, sb_scr)
        softmax_pv(sa_scr, k0, False)
        scores(k0 + 2, sa_scr)
        softmax_pv(sb_scr, k0 + 1, False)
        return carry

    lax.fori_loop(0, qi // 2, pair, 0)

    @pl.when(qi % 2 == 0)
    def _():
        softmax_pv(sa_scr, qi, True)

    @pl.when(qi % 2 == 1)
    def _():
        scores(qi, sb_scr)
        softmax_pv(sa_scr, qi - 1, False)
        softmax_pv(sb_scr, qi, True)

    o_maps = acc_scr[0:dv, :] / acc_scr[dv:dv + 1, :]
    lam = _diff_lambda(lq1_ref, lk1_ref, lq2_ref, lk2_ref, lam_init)
    ot = o_maps[:, :tq] - lam * o_maps[:, tq:]
    ot = ot * lax.rsqrt(jnp.mean(ot * ot, axis=0, keepdims=True) + LN_EPS) * gcol_ref[...] * (1.0 - lam_init)
    o_ref[...] = ot.T.astype(o_ref.dtype)


def _prompt_attention(qt, kb, vt, slopes, lam_vecs, subln_g, *, n, t, h, dk, dv, lam_init):
    tq = qt.shape[2]
    assert tq & (tq - 1) == 0 and t % tq == 0 and dv == 2 * dk and dv >= 2 * BF16_ROWS
    nq = t // tq
    vec = lambda w: pl.BlockSpec((1, w), lambda b, hh, i, s: (0, 0))
    grid_spec = pltpu.PrefetchScalarGridSpec(
        num_scalar_prefetch=1,
        grid=(n, h, nq),
        in_specs=[vec(dk), vec(dk), vec(dk), vec(dk), pl.BlockSpec((dv, 1), lambda b, hh, i, s: (0, 0)),
                  pl.BlockSpec((None, dv, tq), lambda b, hh, i, s: (b * nq + i, hh, 0)),
                  pl.BlockSpec((t, 2 * dv), lambda b, hh, i, s: (b, hh)),
                  pl.BlockSpec((nq, dv, tq), lambda b, hh, i, s: (b, hh, 0))],
        out_specs=pl.BlockSpec((tq, dv), lambda b, hh, i, s: (b * nq + i, hh)),
        scratch_shapes=[pltpu.VMEM((2 * dv, 2 * tq), BF16), pltpu.VMEM((1, 2 * tq), F32),
                        pltpu.VMEM((dv + BF16_ROWS, 2 * tq), F32),
                        pltpu.VMEM((tq, 2 * tq), F32), pltpu.VMEM((tq, 2 * tq), F32)],
    )
    return pl.pallas_call(
        functools.partial(_prompt_attn_kernel, tq=tq, dk=dk, lam_init=lam_init),
        grid_spec=grid_spec,
        out_shape=jax.ShapeDtypeStruct((n * t, h * dv), BF16),
        compiler_params=_params(3),
        name="prompt_attention",
    )(slopes, *lam_vecs, subln_g.reshape(dv, 1), qt, kb, vt)


def _sample_attn_kernel(pt_ref, slopes_ref, lq1_ref, lk1_ref, lq2_ref, lk2_ref, g_ref, q_ref, kn_ref, vn_ref,
                        *rest, n_heads, dk, tnew, page, pages_per_step, past, lam_init):
    del pt_ref
    k_refs = rest[:pages_per_step]
    v_refs = rest[pages_per_step:2 * pages_per_step]
    o_ref = rest[2 * pages_per_step]
    qx_scr, bias_scr, m_scr, l_scr, acc_scr, sa_scr, sb_scr = rest[2 * pages_per_step + 1:]
    j = pl.program_id(1)
    dv = 2 * dk
    hrows = 2 * tnew
    rows = n_heads * hrows
    cols = page * n_heads
    head_shift = int(math.log2(n_heads))
    row_i = lax.broadcasted_iota(jnp.int32, (rows, 1), 0)
    slope_rows = jnp.zeros((rows, 1), F32)
    for hh in range(n_heads):
        slope_rows = jnp.where(row_i // hrows == hh, slopes_ref[hh] * math.log2(math.e), slope_rows)
    nt = (((1,), (1,)), ((), ()))

    @pl.when(j == 0)
    def _():
        lane = lax.broadcasted_iota(jnp.int32, (tnew, dv), 1)
        pieces = []
        for hh in range(n_heads):
            qh = q_ref[:, hh * dv:(hh + 1) * dv]
            pieces += [jnp.where(lane < dk, qh, 0.0), jnp.where(lane >= dk, qh, 0.0)]
        qx_scr[...] = jnp.concatenate(pieces, axis=0).astype(BF16)
        r = lax.broadcasted_iota(jnp.int32, (rows, cols), 0)
        c = lax.broadcasted_iota(jnp.int32, (rows, cols), 1)
        own = (c & (n_heads - 1)) == r // hrows
        bias_scr[...] = jnp.where(own, slope_rows * (c >> head_shift).astype(F32), NEG_BIG)
        m_scr[...] = jnp.full(m_scr.shape, NEG_BIG, F32)
        l_scr[...] = jnp.zeros(l_scr.shape, F32)
        acc_scr[...] = jnp.zeros(acc_scr.shape, F32)

    def update(s_b, row_off, vv):
        m_prev = m_scr[...]
        m_new = jnp.maximum(m_prev, jnp.max(s_b, axis=1, keepdims=True) + row_off)
        alpha = jnp.exp2(m_prev - m_new)
        p = jnp.exp2(s_b + (row_off - m_new))
        l_scr[...] = alpha * l_scr[...] + jnp.sum(p, axis=1, keepdims=True)
        acc_scr[...] = alpha * acc_scr[...] + jnp.dot(p.astype(BF16), vv, preferred_element_type=F32)
        m_scr[...] = m_new

    qx = qx_scr[...]
    s_bufs = (sa_scr, sb_scr)

    def scores(i):
        s_bufs[i % 2][...] = lax.dot_general(qx, k_refs[i][...].astype(BF16), nt, preferred_element_type=F32)

    scores(0)
    for i in range(pages_per_step):
        if i + 1 < pages_per_step:
            scores(i + 1)
        page_off = jnp.full((rows, 1), (j * pages_per_step + i) * page - past, jnp.int32).astype(F32)
        update(s_bufs[i % 2][...] + bias_scr[...], slope_rows * page_off, v_refs[i][...].astype(BF16))

    @pl.when(j == pl.num_programs(1) - 1)
    def _():
        new_cols = tnew * n_heads
        pad = jnp.zeros((LANES - new_cols, dv), F32)
        kn = jnp.concatenate([kn_ref[...], pad], axis=0).astype(BF16)
        vn = jnp.concatenate([vn_ref[...], pad], axis=0).astype(BF16)
        sn = lax.dot_general(qx, kn, nt, preferred_element_type=F32)
        r = lax.broadcasted_iota(jnp.int32, sn.shape, 0)
        c = lax.broadcasted_iota(jnp.int32, sn.shape, 1)
        tcol = c >> head_shift
        ok = ((c & (n_heads - 1)) == r // hrows) & (tcol <= r % tnew) & (c < new_cols)
        sn = jnp.where(ok, sn + slope_rows * tcol.astype(F32), NEG_BIG)
        update(sn, jnp.zeros((rows, 1), F32), vn)
        o_all = acc_scr[...] / l_scr[...]
        lam = _diff_lambda(lq1_ref, lk1_ref, lq2_ref, lk2_ref, lam_init)
        g = g_ref[...]
        for hh in range(n_heads):
            blk = o_all[hh * hrows:(hh + 1) * hrows]
            o = blk[:tnew] - lam * blk[tnew:]
            o_ref[:, hh * dv:(hh + 1) * dv] = _sub_layer_norm(o, g, lam_init).astype(o_ref.dtype)


def _sample_attention(q, kn, vn, cache_k, cache_v, page_table, slopes, lam_vecs, subln_g, *,
                      n_heads, dk, tnew, lam_init, pages_per_step):
    n_seq, n_pages = page_table.shape
    dv = 2 * dk
    width = n_heads * dv
    page = cache_k.shape[1] // n_heads
    assert n_pages % pages_per_step == 0 and tnew % SUBLANES == 0 and cache_k.shape[2] == dv
    assert n_heads & (n_heads - 1) == 0 and tnew * n_heads <= LANES
    rows = n_heads * 2 * tnew
    vec = lambda w: pl.BlockSpec((1, w), lambda b, j, pt, s: (0, 0))
    seq_rows = pl.BlockSpec((tnew, width), lambda b, j, pt, s: (b, 0))
    seq_heads = pl.BlockSpec((tnew * n_heads, dv), lambda b, j, pt, s: (b, 0))

    def page_spec(i):
        return pl.BlockSpec((None, page * n_heads, dv),
                            lambda b, j, pt, s: (pt[b, j * pages_per_step + i], 0, 0))

    grid_spec = pltpu.PrefetchScalarGridSpec(
        num_scalar_prefetch=2,
        grid=(n_seq, n_pages // pages_per_step),
        in_specs=[vec(dk), vec(dk), vec(dk), vec(dk), vec(dv), seq_rows, seq_heads, seq_heads]
        + [page_spec(i) for i in range(pages_per_step)] * 2,
        out_specs=pl.BlockSpec((tnew, width), lambda b, j, pt, s: (b, 0)),
        scratch_shapes=[pltpu.VMEM((rows, dv), BF16), pltpu.VMEM((rows, page * n_heads), F32),
                        pltpu.VMEM((rows, 1), F32), pltpu.VMEM((rows, 1), F32), pltpu.VMEM((rows, dv), F32),
                        pltpu.VMEM((rows, page * n_heads), F32), pltpu.VMEM((rows, page * n_heads), F32)],
    )
    return pl.pallas_call(
        functools.partial(_sample_attn_kernel, n_heads=n_heads, dk=dk, tnew=tnew, page=page,
                          pages_per_step=pages_per_step, past=n_pages * page, lam_init=lam_init),
        grid_spec=grid_spec,
        out_shape=jax.ShapeDtypeStruct((n_seq * tnew, width), F32),
        compiler_params=_params(2),
        name="sample_attention",
    )(page_table, slopes, *lam_vecs, subln_g, q, kn, vn,
      *([cache_k] * pages_per_step), *([cache_v] * pages_per_step))


def _s5_prep_kernel(are_ref, aim_ref, ldt_ref, bre_ref, bim_ref, pwre_ref, pwim_ref, bbre_ref, bbim_ref,
                    *, seg, n_blocks):
    lam_re = jnp.minimum(are_ref[...], -1e-4)
    lam_im = aim_ref[...]
    dt = jnp.exp(ldt_ref[...])
    z_re, z_im = lam_re * dt, lam_im * dt
    ez = jnp.exp(z_re)
    ab_re, ab_im = ez * jnp.cos(z_im), ez * jnp.sin(z_im)
    den = lam_re * lam_re + lam_im * lam_im
    nr, ni = ab_re - 1.0, ab_im
    f_re = (nr * lam_re + ni * lam_im) / den
    f_im = (ni * lam_re - nr * lam_im) / den
    steps = (lax.broadcasted_iota(jnp.int32, (seg, 1), 0) + 1).astype(F32)
    ek = jnp.exp(z_re * steps)
    pwre_ref[...] = ek * jnp.cos(z_im * steps)
    pwim_ref[...] = ek * jnp.sin(z_im * steps)
    w = bbre_ref.shape[2]
    for jb in range(n_blocks):
        fr, fi = f_re[:, jb * w:(jb + 1) * w], f_im[:, jb * w:(jb + 1) * w]
        b_re, b_im = bre_ref[jb], bim_ref[jb]
        bbre_ref[jb] = (fr * b_re - fi * b_im).astype(BF16)
        bbim_ref[jb] = (fr * b_im + fi * b_re).astype(BF16)


def _s5_prep(a_re_row, a_im_row, log_dt_row, b_re_bd, b_im_bd, *, seg):
    n_blocks, kb, wb = b_re_bd.shape
    gp = a_re_row.shape[1]
    return pl.pallas_call(
        functools.partial(_s5_prep_kernel, seg=seg, n_blocks=n_blocks),
        out_shape=[jax.ShapeDtypeStruct((seg, gp), F32), jax.ShapeDtypeStruct((seg, gp), F32),
                   jax.ShapeDtypeStruct((n_blocks, kb, wb), BF16), jax.ShapeDtypeStruct((n_blocks, kb, wb), BF16)],
        compiler_params=pltpu.CompilerParams(vmem_limit_bytes=VMEM_LIMIT_BYTES),
        name="s5_prep",
    )(a_re_row, a_im_row, log_dt_row, b_re_bd, b_im_bd)


def _gelu_tanh(y):
    return 0.5 * y * (1.0 + jnp.tanh(math.sqrt(2.0 / math.pi) * (y + 0.044715 * (y * y * y))))


def _s5_kernel(u_ref, h0re_ref, h0im_ref, pwre_ref, pwim_ref, bbre_ref, bbim_ref, cre_ref, cim_ref, d_ref,
               wglu_ref, bglu_ref, s_ref, hre_ref, him_ref, xre_scr, xim_scr, y_scr, cre_scr, cim_scr,
               *, seg, chained):
    rows = SUBLANES * seg
    n_blocks, kb, wb = bbre_ref.shape
    cb = cre_ref.shape[2]

    ri = lax.broadcasted_iota(jnp.int32, (rows, rows), 0)
    ci = lax.broadcasted_iota(jnp.int32, (rows, rows), 1)
    perm = jnp.where(ci == (ri % SUBLANES) * seg + ri // SUBLANES, 1.0, 0.0).astype(BF16)
    perm_t = jnp.where(ri == (ci % SUBLANES) * seg + ci // SUBLANES, 1.0, 0.0).astype(BF16)

    u = u_ref[...]
    u_hi = u.astype(BF16)
    u_lo = (u - u_hi.astype(F32)).astype(BF16)
    up_hi = jnp.dot(perm, u_hi, preferred_element_type=F32)
    up = up_hi + jnp.dot(perm, u_lo, preferred_element_type=F32)
    up_b = up_hi.astype(BF16)

    if chained:
        @pl.when(pl.program_id(1) == 0)
        def _():
            cre_scr[...] = h0re_ref[0]
            cim_scr[...] = h0im_ref[0]

    for jb in range(n_blocks):
        lanes = slice(jb * wb, (jb + 1) * wb)
        uj = up_b[:, jb * kb:(jb + 1) * kb]
        xre_scr[...] = jnp.dot(uj, bbre_ref[jb], preferred_element_type=F32).reshape(seg, SUBLANES, wb)
        xim_scr[...] = jnp.dot(uj, bbim_ref[jb], preferred_element_type=F32).reshape(seg, SUBLANES, wb)
        a_re = jnp.broadcast_to(pwre_ref[0:1, lanes], (SUBLANES, wb))
        a_im = jnp.broadcast_to(pwim_ref[0:1, lanes], (SUBLANES, wb))

        def scan_step(t, state):
            xr, xi = state
            nr = a_re * xr - a_im * xi + xre_scr[t]
            ni = a_re * xi + a_im * xr + xim_scr[t]
            xre_scr[t] = nr
            xim_scr[t] = ni
            return nr, ni

        zeros = jnp.zeros((SUBLANES, wb), F32)
        end_re, end_im = lax.fori_loop(0, seg, scan_step, (zeros, zeros), unroll=True)

        if chained:
            as_re, as_im = pwre_ref[seg - 1:seg, lanes], pwim_ref[seg - 1:seg, lanes]
            c_re, c_im = cre_scr[:, lanes], cim_scr[:, lanes]
            seg_row = lax.broadcasted_iota(jnp.int32, (SUBLANES, wb), 0)
            in_re = jnp.broadcast_to(c_re, (SUBLANES, wb))
            in_im = jnp.broadcast_to(c_im, (SUBLANES, wb))
            for r in range(1, SUBLANES + 1):
                e_re, e_im = end_re[r - 1:r], end_im[r - 1:r]
                c_re, c_im = e_re + as_re * c_re - as_im * c_im, e_im + as_re * c_im + as_im * c_re
                if r < SUBLANES:
                    in_re = jnp.where(seg_row == r, c_re, in_re)
                    in_im = jnp.where(seg_row == r, c_im, in_im)
            cre_scr[:, lanes] = c_re
            cim_scr[:, lanes] = c_im
            hre_ref[0, :, lanes] = c_re
            him_ref[0, :, lanes] = c_im
        else:
            in_re, in_im = h0re_ref[:, lanes], h0im_ref[:, lanes]

        def fix_step(t, carry):
            p_re = pwre_ref[pl.ds(t, 1), lanes]
            p_im = pwim_ref[pl.ds(t, 1), lanes]
            xre_scr[t] = xre_scr[t] + p_re * in_re - p_im * in_im
            xim_scr[t] = xim_scr[t] + p_re * in_im + p_im * in_re
            return carry

        lax.fori_loop(0, seg, fix_step, 0, unroll=True)
        if not chained:
            hre_ref[:, lanes] = xre_scr[seg - 1]
            him_ref[:, lanes] = xim_scr[seg - 1]
        x_re = xre_scr[...].reshape(rows, wb).astype(BF16)
        x_im = xim_scr[...].reshape(rows, wb).astype(BF16)
        y_scr[:, jb * cb:(jb + 1) * cb] = (jnp.dot(x_re, cre_ref[jb], preferred_element_type=F32)
                                           - jnp.dot(x_im, cim_ref[jb], preferred_element_type=F32))

    y = y_scr[...] + d_ref[...] * up
    g = _gelu_tanh(y)
    gate = jax.nn.sigmoid(jnp.dot(g.astype(BF16), wglu_ref[...], preferred_element_type=F32) + bglu_ref[...])
    s_perm = (g * gate).astype(BF16)
    s_ref[...] = jnp.dot(perm_t, s_perm, preferred_element_type=F32).astype(s_ref.dtype)


def _s5(u, h0_re, h0_im, pw_re, pw_im, bb_re, bb_im, c_re_bd, c_im_bd, d_row, w_glu_b, b_glu_row,
        *, n_seq, t, seg, chained):
    rows = SUBLANES * seg
    ssm_w = u.shape[1]
    gp = pw_re.shape[1]
    n_blocks, kb, wb = bb_re.shape
    pw = pw_re[:seg], pw_im[:seg]
    consts = [_resident((seg, gp)), _resident((seg, gp)), _resident(bb_re.shape), _resident(bb_im.shape),
              _resident(c_re_bd.shape), _resident(c_im_bd.shape), _resident(d_row.shape),
              _resident(w_glu_b.shape), _resident(b_glu_row.shape)]
    scratch = [pltpu.VMEM((seg, SUBLANES, wb), F32), pltpu.VMEM((seg, SUBLANES, wb), F32),
               pltpu.VMEM((rows, ssm_w), F32), pltpu.VMEM((1, gp), F32), pltpu.VMEM((1, gp), F32)]
    if chained:
        assert t % rows == 0
        nc = t // rows
        grid = (n_seq, nc)
        u_spec = pl.BlockSpec((rows, ssm_w), lambda b, c: (b * nc + c, 0))
        h_spec = pl.BlockSpec((1, 1, gp), lambda b, c: (b, 0, 0))
        h_shape = jax.ShapeDtypeStruct((n_seq, 1, gp), F32)
    else:
        assert t == seg and n_seq % SUBLANES == 0
        grid = (n_seq // SUBLANES,)
        u_spec = pl.BlockSpec((rows, ssm_w), lambda b: (b, 0))
        h_spec = pl.BlockSpec((SUBLANES, gp), lambda b: (b, 0))
        h_shape = jax.ShapeDtypeStruct((n_seq, gp), F32)
    return pl.pallas_call(
        functools.partial(_s5_kernel, seg=seg, chained=chained),
        grid=grid,
        in_specs=[u_spec, h_spec, h_spec] + consts,
        out_specs=[u_spec, h_spec, h_spec],
        out_shape=[jax.ShapeDtypeStruct(u.shape, BF16), h_shape, h_shape],
        scratch_shapes=scratch,
        compiler_params=_params(len(grid)),
        name="s5_chained" if chained else "s5_batched",
    )(u, h0_re, h0_im, *pw, bb_re, bb_im, c_re_bd, c_im_bd, d_row, w_glu_b, b_glu_row)


def _layer_norm(x, g, b):
    mu = jnp.mean(x, axis=-1, keepdims=True)
    xc = x - mu
    var = jnp.mean(xc * xc, axis=-1, keepdims=True)
    return xc * lax.rsqrt(var + LN_EPS) * g + b


def _out_proj_kernel(x_ref, o_ref, s_ref, wo_ref, ws_ref, g_ref, b_ref, x1_ref, x1b_ref, *, alpha):
    mix = (jnp.dot(o_ref[...].astype(BF16), wo_ref[...], preferred_element_type=F32)
           + jnp.dot(s_ref[...].astype(BF16), ws_ref[...], preferred_element_type=F32))
    x1 = _layer_norm(alpha * x_ref[...] + mix, g_ref[...], b_ref[...])
    x1_ref[...] = x1
    x1b_ref[...] = x1.astype(BF16)


def _out_proj(x, o, s, w_o, w_s, g_row, b_row, *, alpha, tm):
    rows, d = x.shape
    row_block = lambda w: pl.BlockSpec((tm, w), lambda i: (i, 0))
    return pl.pallas_call(
        functools.partial(_out_proj_kernel, alpha=alpha),
        grid=(rows // tm,),
        in_specs=[row_block(d), row_block(o.shape[1]), row_block(s.shape[1]), _resident(w_o.shape),
                  _resident(w_s.shape), _resident(g_row.shape), _resident(b_row.shape)],
        out_specs=[row_block(d), row_block(d)],
        out_shape=[jax.ShapeDtypeStruct((rows, d), F32), jax.ShapeDtypeStruct((rows, d), BF16)],
        compiler_params=_params(1),
        name="out_proj_ln",
    )(x, o, s, w_o, w_s, g_row, b_row)


def _ffn_kernel(x1_ref, x1b_ref, wup_ref, wdn_ref, g_ref, b_ref, x2_ref, acc_scr, *, alpha):
    f = pl.program_id(1)

    @pl.when(f == 0)
    def _():
        acc_scr[...] = jnp.zeros(acc_scr.shape, F32)

    hidden = jnp.maximum(jnp.dot(x1b_ref[...], wup_ref[...], preferred_element_type=F32), 0.0)
    hidden = (hidden * hidden).astype(BF16)
    acc_scr[...] += jnp.dot(hidden, wdn_ref[...], preferred_element_type=F32)

    @pl.when(f == pl.num_programs(1) - 1)
    def _():
        x2_ref[...] = _layer_norm(alpha * x1_ref[...] + acc_scr[...], g_ref[...], b_ref[...])


def _ffn(x1, x1b, w_up, w_dn, g_row, b_row, *, alpha, tm, tf):
    rows, d = x1.shape
    dff = w_up.shape[1]
    return pl.pallas_call(
        functools.partial(_ffn_kernel, alpha=alpha),
        grid=(rows // tm, dff // tf),
        in_specs=[pl.BlockSpec((tm, d), lambda i, f: (i, 0)), pl.BlockSpec((tm, d), lambda i, f: (i, 0)),
                  pl.BlockSpec((d, tf), lambda i, f: (0, f)), pl.BlockSpec((tf, d), lambda i, f: (f, 0)),
                  _resident(g_row.shape), _resident(b_row.shape)],
        out_specs=pl.BlockSpec((tm, d), lambda i, f: (i, 0)),
        out_shape=jax.ShapeDtypeStruct((rows, d), F32),
        scratch_shapes=[pltpu.VMEM((tm, d), F32)],
        compiler_params=_params(2),
        name="ffn_ln",
    )(x1, x1b, w_up, w_dn, g_row, b_row)


def _block_diag(x, groups_per_block):
    g, r, c = x.shape
    nb = g // groups_per_block
    eye = jnp.eye(groups_per_block, dtype=x.dtype)
    x = x.reshape(nb, groups_per_block, r, c)
    return jnp.einsum("jlrc,lm->jlrmc", x, eye).reshape(nb, groups_per_block * r, groups_per_block * c)


def _pick_tile(rows, target):
    t = min(rows, target)
    while rows % t:
        t //= 2
    return t


def _layer(x, attend, s5_run, prm, *, alpha, dims, seq_len, for_flash):
    tm = _pick_tile(x.shape[0], 512)
    q_scale = dims["dk"] ** -0.5 * math.log2(math.e)
    outs = _proj_in(x, prm["w_in"], qk_w=dims["qk_w"], att_w=dims["att_w"], n_heads=dims["n_heads"],
                    q_scale=q_scale, tm=tm, for_flash=for_flash, seq_len=seq_len)
    if for_flash:
        qt, k, v, kb, vt, u = outs
        o = attend(qt, kb, vt)
    else:
        q, k, v, u = outs
        o = attend(q, k, v)
    s, h_re, h_im = s5_run(u)
    x1, x1b = _out_proj(x, o, s, prm["w_out_att"], prm["w_out_ssm"], prm["ln1_g"], prm["ln1_b"], alpha=alpha, tm=tm)
    x2 = _ffn(x1, x1b, prm["w_up"], prm["w_down"], prm["ln2_g"], prm["ln2_b"], alpha=alpha, tm=tm,
              tf=_pick_tile(prm["w_up"].shape[1], 1024))
    return x2, k, v, h_re, h_im


def kernel(x_prompt, x_sample, cache_k, cache_v, state_ssm_re, state_ssm_im, page_table, w_in, lambda_q1, lambda_k1, lambda_q2, lambda_k2, subln_g, ssm_a_re, ssm_a_im, ssm_log_dt, ssm_b_re, ssm_b_im, ssm_c_re, ssm_c_im, ssm_d, w_glu, b_glu, w_out, ln1_g, ln1_b, w_up, w_down, ln2_g, ln2_b):
    depth, d_model, _ = w_in.shape
    n_p, t_p, _ = x_prompt.shape
    n_s, t_s, _ = x_sample.shape
    _, n_pool, page, n_heads, two_dk = cache_k.shape
    dk = two_dk // 2
    dv = cache_v.shape[-1]
    qk_w, att_w = n_heads * two_dk, n_heads * dv
    ssm_w = w_in.shape[2] - 2 * qk_w - att_w
    n_groups, n_states, n_chan = ssm_b_re.shape[1:]
    gp = n_groups * n_states
    gpb = 16
    assert dv == two_dk and ssm_w == n_groups * n_chan
    dims = dict(qk_w=qk_w, att_w=att_w, dk=dk, n_heads=n_heads)
    alpha = (2.0 * depth) ** 0.25
    slopes = jnp.exp2(-8.0 * jnp.arange(1, n_heads + 1, dtype=F32) / n_heads)
    seg_p = 32
    assert t_s <= seg_p

    yp = x_prompt.reshape(n_p * t_p, d_model)
    ys = x_sample.reshape(n_s * t_s, d_model)
    zeros_state = jnp.zeros((n_p, 1, gp), F32)
    outs = [[] for _ in range(8)]
    for l in range(depth):
        lam_init = 0.8 - 0.6 * math.exp(-0.3 * l)
        row = lambda a: a[l].reshape(1, -1)
        prm = dict(
            w_in=w_in[l].astype(BF16), w_out_att=w_out[l, :att_w].astype(BF16),
            w_out_ssm=w_out[l, att_w:].astype(BF16), w_up=w_up[l].astype(BF16), w_down=w_down[l].astype(BF16),
            ln1_g=row(ln1_g), ln1_b=row(ln1_b), ln2_g=row(ln2_g), ln2_b=row(ln2_b))
        lam_vecs = (row(lambda_q1), row(lambda_k1), row(lambda_q2), row(lambda_k2))
        g_row = row(subln_g)

        pw_re, pw_im, bb_re, bb_im = _s5_prep(
            row(ssm_a_re), row(ssm_a_im), jnp.repeat(ssm_log_dt[l], n_states).reshape(1, gp),
            _block_diag(jnp.swapaxes(ssm_b_re[l], 1, 2), gpb), _block_diag(jnp.swapaxes(ssm_b_im[l], 1, 2), gpb),
            seg=seg_p)
        s5_consts = (pw_re, pw_im, bb_re, bb_im,
                     _block_diag(jnp.swapaxes(ssm_c_re[l], 1, 2), gpb).astype(BF16),
                     _block_diag(jnp.swapaxes(ssm_c_im[l], 1, 2), gpb).astype(BF16),
                     row(ssm_d), w_glu[l].astype(BF16), row(b_glu))

        p_att = lambda qt, kb, vt: _prompt_attention(
            qt, kb, vt, slopes, lam_vecs, g_row, n=n_p, t=t_p, h=n_heads, dk=dk, dv=dv, lam_init=lam_init)
        p_s5 = lambda u: _s5(u, zeros_state, zeros_state, *s5_consts, n_seq=n_p, t=t_p, seg=seg_p, chained=True)
        yp, kp, vp, hrp, hip = _layer(yp, p_att, p_s5, prm, alpha=alpha, dims=dims, seq_len=t_p, for_flash=True)

        ck = cache_k[l].reshape(n_pool, page * n_heads, two_dk)
        cv = cache_v[l].reshape(n_pool, page * n_heads, dv)
        s_att = lambda q, k, v: _sample_attention(
            q, k, v, ck, cv, page_table, slopes, lam_vecs, g_row, n_heads=n_heads, dk=dk, tnew=t_s,
            lam_init=lam_init, pages_per_step=8)
        s_s5 = lambda u: _s5(u, state_ssm_re[l].reshape(n_s, gp), state_ssm_im[l].reshape(n_s, gp), *s5_consts,
                             n_seq=n_s, t=t_s, seg=t_s, chained=False)
        ys, ksm, vsm, hrs, his = _layer(ys, s_att, s_s5, prm, alpha=alpha, dims=dims, seq_len=t_s, for_flash=False)

        for acc, val in zip(outs, (
                kp.reshape(n_p, t_p, n_heads, two_dk), vp.reshape(n_p, t_p, n_heads, dv),
                hrp.reshape(n_p, n_groups, n_states), hip.reshape(n_p, n_groups, n_states),
                ksm.reshape(n_s, t_s, n_heads, two_dk), vsm.reshape(n_s, t_s, n_heads, dv),
                hrs.reshape(n_s, n_groups, n_states), his.reshape(n_s, n_groups, n_states))):
            acc.append(val)
    return (yp.reshape(x_prompt.shape), ys.reshape(x_sample.shape), *(jnp.stack(o) for o in outs))
```
